```python
import jax, jax.numpy as jnp
from jax import lax
import numpy as np

D_MODEL = 1024
BATCH = 8
SEQ = 2048
DEPTH = 1
DEC_BATCH = 128
DEC_SEQ = 1
PAST_LEN = 16384
PAGE_SIZE = 128

D_MIX = D_MODEL
HEAD_DIM = 64
D_A = D_MIX // 2
D_B = D_MIX - D_A
N_GROUPS_A = D_A // HEAD_DIM
N_GROUPS_B = D_B // HEAD_DIM
K_A = 31
K_B = 3
D_FF = 2816
D_IN = 2 * D_A + 3 * D_B
LN_EPS = 1e-5
ALPHA = (2 * DEPTH) ** 0.25
BETA = (8 * DEPTH) ** -0.25

kernel_name = "hybrid_conformer_shortconv_decoder_step"


def layer_norm(x, g, b):
    xf = x.astype(jnp.float32)
    mu = jnp.mean(xf, axis=-1, keepdims=True)
    var = jnp.mean(jnp.square(xf - mu), axis=-1, keepdims=True)
    return ((xf - mu) * lax.rsqrt(var + LN_EPS)).astype(x.dtype) * g + b


def swiglu(x, w_gate, w_up, w_down):
    return (jax.nn.silu(x @ w_gate) * (x @ w_up)) @ w_down


def causal_depthwise(ctx, new, w):
    k = w.shape[0]
    full = jnp.concatenate([ctx, new], axis=1)
    out = lax.conv_general_dilated(
        full, w[:, None, :], window_strides=(1,), padding='VALID',
        dimension_numbers=('NWC', 'WIO', 'NWC'), feature_group_count=w.shape[1])
    return out, full[:, full.shape[1] - (k - 1):]


def token_mixers(h, ctx_a, ctx_b, w_in, b_in, w_dw_a, b_dw_a, ln_conv_g, ln_conv_b, w_dw_b, w_o, b_o):
    z = h @ w_in + b_in
    a_val, a_gate, g_b, g_c, b_val = jnp.split(
        z, [D_A, 2 * D_A, 2 * D_A + D_B, 2 * D_A + 2 * D_B], axis=-1)
    u = a_val * jax.nn.sigmoid(a_gate)
    ua, new_ctx_a = causal_depthwise(ctx_a, u, w_dw_a)
    ua = jax.nn.silu(layer_norm(ua + b_dw_a, ln_conv_g, ln_conv_b))
    v = g_c * b_val
    vb, new_ctx_b = causal_depthwise(ctx_b, v, w_dw_b)
    ub = g_b * vb
    out = jnp.concatenate([ua, ub], axis=-1) @ w_o + b_o
    return out, new_ctx_a, new_ctx_b


def trunk_layer(x, ctx_a, ctx_b, ln_f1_g, ln_f1_b, f1_wg, f1_wu, f1_wd,
                w_in, b_in, w_dw_a, b_dw_a, ln_conv_g, ln_conv_b, w_dw_b, w_o, b_o,
                ln_mix_g, ln_mix_b, f2_wg, f2_wu, f2_wd, ln_f2_g, ln_f2_b):
    x = layer_norm(ALPHA * x + 0.5 * swiglu(x, f1_wg, f1_wu, f1_wd), ln_f1_g, ln_f1_b)
    m, new_a, new_b = token_mixers(x, ctx_a, ctx_b, w_in, b_in, w_dw_a, b_dw_a,
                                   ln_conv_g, ln_conv_b, w_dw_b, w_o, b_o)
    x = layer_norm(ALPHA * x + m, ln_mix_g, ln_mix_b)
    x = layer_norm(ALPHA * x + 0.5 * swiglu(x, f2_wg, f2_wu, f2_wd), ln_f2_g, ln_f2_b)
    return x, new_a, new_b


def setup_inputs(seed: int = 0) -> dict:
    key = jax.random.key(seed)
    ks = iter(jax.random.split(key, 32))
    f32 = jnp.float32

    def nrm(shape, scale):
        return jax.random.normal(next(ks), shape, f32) * scale

    def gain(shape):
        return 1.0 + nrm(shape, 0.02)

    L = DEPTH
    return {
        "x_prompt": nrm((BATCH, SEQ, D_MODEL), 1.0),
        "x_sample": nrm((DEC_BATCH, DEC_SEQ, D_MODEL), 1.0),
        "state_conv_a": nrm((L, DEC_BATCH, K_A - 1, D_A), 0.5),
        "state_conv_b": nrm((L, DEC_BATCH, K_B - 1, D_B), 0.5),
        "ln_f1_g": gain((L, D_MODEL)),
        "ln_f1_b": nrm((L, D_MODEL), 0.02),
        "f1_wg": nrm((L, D_MODEL, D_FF), D_MODEL ** -0.5),
        "f1_wu": nrm((L, D_MODEL, D_FF), D_MODEL ** -0.5),
        "f1_wd": nrm((L, D_FF, D_MODEL), BETA * D_FF ** -0.5),
        "w_in": nrm((L, D_MODEL, D_IN), D_MODEL ** -0.5),
        "b_in": nrm((L, D_IN), 0.02),
        "w_dw_a": nrm((L, K_A, D_A), K_A ** -0.5),
        "b_dw_a": nrm((L, D_A), 0.02),
        "ln_conv_g": gain((L, D_A)),
        "ln_conv_b": nrm((L, D_A), 0.02),
        "w_dw_b": nrm((L, K_B, D_B), K_B ** -0.5),
        "w_o": nrm((L, D_MIX, D_MODEL), BETA * D_MIX ** -0.5),
        "b_o": nrm((L, D_MODEL), 0.02),
        "ln_mix_g": gain((L, D_MODEL)),
        "ln_mix_b": nrm((L, D_MODEL), 0.02),
        "f2_wg": nrm((L, D_MODEL, D_FF), D_MODEL ** -0.5),
        "f2_wu": nrm((L, D_MODEL, D_FF), D_MODEL ** -0.5),
        "f2_wd": nrm((L, D_FF, D_MODEL), BETA * D_FF ** -0.5),
        "ln_f2_g": gain((L, D_MODEL)),
        "ln_f2_b": nrm((L, D_MODEL), 0.02),
    }


def reference(x_prompt, x_sample, state_conv_a, state_conv_b,
              ln_f1_g, ln_f1_b, f1_wg, f1_wu, f1_wd,
              w_in, b_in, w_dw_a, b_dw_a, ln_conv_g, ln_conv_b, w_dw_b, w_o, b_o,
              ln_mix_g, ln_mix_b, f2_wg, f2_wu, f2_wd, ln_f2_g, ln_f2_b):
    yp, ys = x_prompt, x_sample
    pa_list, pb_list, sa_list, sb_list = [], [], [], []
    for l in range(DEPTH):
        params = (ln_f1_g[l], ln_f1_b[l], f1_wg[l], f1_wu[l], f1_wd[l],
                  w_in[l], b_in[l], w_dw_a[l], b_dw_a[l], ln_conv_g[l], ln_conv_b[l],
                  w_dw_b[l], w_o[l], b_o[l], ln_mix_g[l], ln_mix_b[l],
                  f2_wg[l], f2_wu[l], f2_wd[l], ln_f2_g[l], ln_f2_b[l])
        ctx_a0 = jnp.zeros((yp.shape[0], K_A - 1, D_A), yp.dtype)
        ctx_b0 = jnp.zeros((yp.shape[0], K_B - 1, D_B), yp.dtype)
        yp, pa, pb = trunk_layer(yp, ctx_a0, ctx_b0, *params)
        ys, sa, sb = trunk_layer(ys, state_conv_a[l], state_conv_b[l], *params)
        pa_list.append(pa); pb_list.append(pb); sa_list.append(sa); sb_list.append(sb)
    new_conv_a_prompt = jnp.stack(pa_list)
    new_conv_b_prompt = jnp.stack(pb_list)
    new_conv_a_sample = jnp.stack(sa_list)
    new_conv_b_sample = jnp.stack(sb_list)
    return (yp, ys, new_conv_a_prompt, new_conv_b_prompt, new_conv_a_sample, new_conv_b_sample)
```

```python
import functools

import jax
import jax.numpy as jnp
from jax.experimental import pallas as pl
from jax.experimental.pallas import tpu as pltpu

D_MODEL = 1024
D_FF = 2816
D_A = 512
D_B = 512
D_IN = 2 * D_A + 3 * D_B
K_A = 31
K_B = 3
LN_EPS = 1e-5
DEPTH = 1
ALPHA = (2 * DEPTH) ** 0.25

SUBLANES = 8
PAD_A = 32
PAD_B = 8
VMEM_LIMIT_BYTES = 56 * 1024 * 1024


def _layer_norm(x, g, b):
    mu = jnp.mean(x, axis=-1, keepdims=True)
    xc = x - mu
    var = jnp.mean(xc * xc, axis=-1, keepdims=True)
    return xc * jax.lax.rsqrt(var + LN_EPS) * g + b


def _bdot(a, w):
    return jnp.dot(a, w, preferred_element_type=jnp.float32)


def _ffn_kernel(x_ref, wg_ref, wu_ref, wd_ref, g_ref, b_ref, o_ref):
    x = x_ref[...]
    xb = x.astype(jnp.bfloat16)
    gate = _bdot(xb, wg_ref[...])
    up = _bdot(xb, wu_ref[...])
    h = (jax.nn.silu(gate) * up).astype(jnp.bfloat16)
    d = _bdot(h, wd_ref[...])
    o_ref[...] = _layer_norm(ALPHA * x + 0.5 * d, g_ref[...], b_ref[...])


def _const_spec(shape):
    return pl.BlockSpec(shape, lambda i: (0,) * len(shape), pipeline_mode=pl.Buffered(1))


def _ffn(x, wg, wu, wd, g, b, tm):
    m = x.shape[0]
    assert m % tm == 0
    return pl.pallas_call(
        _ffn_kernel,
        grid=(m // tm,),
        in_specs=[
            pl.BlockSpec((tm, D_MODEL), lambda i: (i, 0)),
            _const_spec((D_MODEL, D_FF)),
            _const_spec((D_MODEL, D_FF)),
            _const_spec((D_FF, D_MODEL)),
            _const_spec((1, D_MODEL)),
            _const_spec((1, D_MODEL)),
        ],
        out_specs=pl.BlockSpec((tm, D_MODEL), lambda i: (i, 0)),
        out_shape=jax.ShapeDtypeStruct((m, D_MODEL), jnp.float32),
        compiler_params=pltpu.CompilerParams(
            dimension_semantics=("arbitrary",), vmem_limit_bytes=VMEM_LIMIT_BYTES),
        name="ffn_ln",
    )(x, wg, wu, wd, g, b)


def _mix_prompt_kernel(x_ref, win_ref, bin_ref, wa_ref, ba_ref, lcg_ref, lcb_ref,
                       wb_ref, wo_ref, bo_ref, g_ref, b_ref,
                       o_ref, ctxa_ref, ctxb_ref, ubuf, vbuf, *, tm, tiles_per_seq):
    i = pl.program_id(0)

    @pl.when(i % tiles_per_seq == 0)
    def _():
        ubuf[pl.ds(0, PAD_A), :] = jnp.zeros((PAD_A, D_A), jnp.float32)
        vbuf[pl.ds(0, PAD_B), :] = jnp.zeros((PAD_B, D_B), jnp.float32)

    x = x_ref[...]
    z = _bdot(x.astype(jnp.bfloat16), win_ref[...]) + bin_ref[...]
    a_val = z[:, 0:D_A]
    a_gate = z[:, D_A:2 * D_A]
    g_b = z[:, 2 * D_A:2 * D_A + D_B]
    g_c = z[:, 2 * D_A + D_B:2 * D_A + 2 * D_B]
    b_val = z[:, 2 * D_A + 2 * D_B:]
    ubuf[pl.ds(PAD_A, tm), :] = a_val * jax.nn.sigmoid(a_gate)
    vbuf[pl.ds(PAD_B, tm), :] = g_c * b_val

    conv_a = jnp.zeros((tm, D_A), jnp.float32)
    for k in range(K_A):
        conv_a = conv_a + wa_ref[k:k + 1, :] * ubuf[pl.ds(PAD_A - (K_A - 1) + k, tm), :]
    ua = jax.nn.silu(_layer_norm(conv_a + ba_ref[...], lcg_ref[...], lcb_ref[...]))

    conv_b = jnp.zeros((tm, D_B), jnp.float32)
    for k in range(K_B):
        conv_b = conv_b + wb_ref[k:k + 1, :] * vbuf[pl.ds(PAD_B - (K_B - 1) + k, tm), :]
    ub = g_b * conv_b

    mixed = jnp.concatenate([ua, ub], axis=-1).astype(jnp.bfloat16)
    m = _bdot(mixed, wo_ref[...]) + bo_ref[...]
    o_ref[...] = _layer_norm(ALPHA * x + m, g_ref[...], b_ref[...])

    ctxa_ref[0, 0] = ubuf[pl.ds(PAD_A + tm - (K_A - 1), K_A - 1), :]
    ctxb_ref[0, 0] = vbuf[pl.ds(PAD_B + tm - (K_B - 1), K_B - 1), :]
    ubuf[pl.ds(0, PAD_A), :] = ubuf[pl.ds(tm, PAD_A), :]
    vbuf[pl.ds(0, PAD_B), :] = vbuf[pl.ds(tm, PAD_B), :]


def _mix_prompt(x, n_seq, seq, w_in, b_in, w_dw_a, b_dw_a, lcg, lcb, w_dw_b, w_o, b_o, g, b, tm):
    assert seq % tm == 0 and tm >= PAD_A
    tiles_per_seq = seq // tm
    kern = functools.partial(_mix_prompt_kernel, tm=tm, tiles_per_seq=tiles_per_seq)
    return pl.pallas_call(
        kern,
        grid=(n_seq * tiles_per_seq,),
        in_specs=[
            pl.BlockSpec((tm, D_MODEL), lambda i: (i, 0)),
            _const_spec((D_MODEL, D_IN)),
            _const_spec((1, D_IN)),
            _const_spec((K_A, D_A)),
            _const_spec((1, D_A)),
            _const_spec((1, D_A)),
            _const_spec((1, D_A)),
            _const_spec((K_B, D_B)),
            _const_spec((D_A + D_B, D_MODEL)),
            _const_spec((1, D_MODEL)),
            _const_spec((1, D_MODEL)),
            _const_spec((1, D_MODEL)),
        ],
        out_specs=[
            pl.BlockSpec((tm, D_MODEL), lambda i: (i, 0)),
            pl.BlockSpec((1, 1, K_A - 1, D_A), lambda i: (0, i // tiles_per_seq, 0, 0)),
            pl.BlockSpec((1, 1, K_B - 1, D_B), lambda i: (0, i // tiles_per_seq, 0, 0)),
        ],
        out_shape=[
            jax.ShapeDtypeStruct((n_seq * seq, D_MODEL), jnp.float32),
            jax.ShapeDtypeStruct((1, n_seq, K_A - 1, D_A), jnp.float32),
            jax.ShapeDtypeStruct((1, n_seq, K_B - 1, D_B), jnp.float32),
        ],
        scratch_shapes=[
            pltpu.VMEM((PAD_A + tm, D_A), jnp.float32),
            pltpu.VMEM((PAD_B + tm, D_B), jnp.float32),
        ],
        compiler_params=pltpu.CompilerParams(
            dimension_semantics=("arbitrary",), vmem_limit_bytes=VMEM_LIMIT_BYTES),
        name="mix_prompt",
    )(x, w_in, b_in, w_dw_a, b_dw_a, lcg, lcb, w_dw_b, w_o, b_o, g, b)


def _mix_sample_kernel(x_ref, sa_ref, sb_ref, win_ref, bin_ref, wa_ref, ba_ref, lcg_ref, lcb_ref,
                       wb_ref, wo_ref, bo_ref, g_ref, b_ref,
                       o_ref, nsa_ref, nsb_ref):
    x = x_ref[...]
    z = _bdot(x.astype(jnp.bfloat16), win_ref[...]) + bin_ref[...]
    a_val = z[:, 0:D_A]
    a_gate = z[:, D_A:2 * D_A]
    g_b = z[:, 2 * D_A:2 * D_A + D_B]
    g_c = z[:, 2 * D_A + D_B:2 * D_A + 2 * D_B]
    b_val = z[:, 2 * D_A + 2 * D_B:]
    u = a_val * jax.nn.sigmoid(a_gate)
    v = g_c * b_val

    sa = sa_ref[0]
    conv_a = jnp.sum(sa * wa_ref[0:K_A - 1, :][None], axis=1) + wa_ref[K_A - 1:K_A, :] * u
    ua = jax.nn.silu(_layer_norm(conv_a + ba_ref[...], lcg_ref[...], lcb_ref[...]))

    sb = sb_ref[0]
    conv_b = jnp.sum(sb * wb_ref[0:K_B - 1, :][None], axis=1) + wb_ref[K_B - 1:K_B, :] * v
    ub = g_b * conv_b

    mixed = jnp.concatenate([ua, ub], axis=-1).astype(jnp.bfloat16)
    m = _bdot(mixed, wo_ref[...]) + bo_ref[...]
    o_ref[...] = _layer_norm(ALPHA * x + m, g_ref[...], b_ref[...])

    nsa_ref[0, :, 0:K_A - 2, :] = sa[:, 1:, :]
    nsa_ref[0, :, K_A - 2:K_A - 1, :] = u[:, None, :]
    nsb_ref[0, :, 0:K_B - 2, :] = sb[:, 1:, :]
    nsb_ref[0, :, K_B - 2:K_B - 1, :] = v[:, None, :]


def _mix_sample(x, state_a, state_b, w_in, b_in, w_dw_a, b_dw_a, lcg, lcb, w_dw_b, w_o, b_o, g, b):
    n = x.shape[0]
    vmem = pl.BlockSpec(memory_space=pltpu.VMEM)
    return pl.pallas_call(
        _mix_sample_kernel,
        in_specs=[vmem] * 14,
        out_specs=[vmem] * 3,
        out_shape=[
            jax.ShapeDtypeStruct((n, D_MODEL), jnp.float32),
            jax.ShapeDtypeStruct(state_a.shape, jnp.float32),
            jax.ShapeDtypeStruct(state_b.shape, jnp.float32),
        ],
        compiler_params=pltpu.CompilerParams(vmem_limit_bytes=VMEM_LIMIT_BYTES),
        name="mix_sample",
    )(x, state_a, state_b, w_in, b_in, w_dw_a, b_dw_a, lcg, lcb, w_dw_b, w_o, b_o, g, b)


def kernel(x_prompt, x_sample, state_conv_a, state_conv_b, ln_f1_g, ln_f1_b, f1_wg, f1_wu, f1_wd,
           w_in, b_in, w_dw_a, b_dw_a, ln_conv_g, ln_conv_b, w_dw_b, w_o, b_o,
           ln_mix_g, ln_mix_b, f2_wg, f2_wu, f2_wd, ln_f2_g, ln_f2_b):
    assert f1_wg.shape[0] == DEPTH == 1
    n_seq, seq, _ = x_prompt.shape
    n_dec = x_sample.shape[0]
    bf = jnp.bfloat16
    f1 = (f1_wg[0].astype(bf), f1_wu[0].astype(bf), f1_wd[0].astype(bf), ln_f1_g, ln_f1_b)
    f2 = (f2_wg[0].astype(bf), f2_wu[0].astype(bf), f2_wd[0].astype(bf), ln_f2_g, ln_f2_b)
    mix = (w_in[0].astype(bf), b_in, w_dw_a[0], b_dw_a, ln_conv_g, ln_conv_b, w_dw_b[0],
           w_o[0].astype(bf), b_o, ln_mix_g, ln_mix_b)

    tm = 512
    yp = x_prompt.reshape(n_seq * seq, D_MODEL)
    yp = _ffn(yp, *f1, tm)
    yp, pa, pb = _mix_prompt(yp, n_seq, seq, *mix, tm)
    yp = _ffn(yp, *f2, tm)

    ys = x_sample.reshape(n_dec, D_MODEL)
    ys = _ffn(ys, *f1, n_dec)
    ys, sa, sb = _mix_sample(ys, state_conv_a, state_conv_b, *mix)
    ys = _ffn(ys, *f2, n_dec)

    return (yp.reshape(n_seq, seq, D_MODEL), ys.reshape(n_dec, 1, D_MODEL), pa, pb, sa, sb)
```

```python
import functools

import jax
import jax.numpy as jnp
from jax.experimental import pallas as pl
from jax.experimental.pallas import tpu as pltpu

D_MODEL = 1024
D_FF = 2816
D_A = 512
D_B = 512
D_IN = 2 * D_A + 3 * D_B
K_A = 31
K_B = 3
LN_EPS = 1e-5
DEPTH = 1
ALPHA = (2 * DEPTH) ** 0.25

SUBLANES = 8
PAD_A = 32
PAD_B = 8
VMEM_LIMIT_BYTES = 56 * 1024 * 1024


def _layer_norm(x, g, b):
    mu = jnp.mean(x, axis=-1, keepdims=True)
    xc = x - mu
    var = jnp.mean(xc * xc, axis=-1, keepdims=True)
    return xc * jax.lax.rsqrt(var + LN_EPS) * g + b


def _bdot(a, w):
    return jnp.dot(a, w, preferred_element_type=jnp.float32)


def _ffn_kernel(x_ref, wg_ref, wu_ref, wd_ref, g_ref, b_ref, o_ref, *, n_sub):
    rows = x_ref.shape[0] // n_sub
    for s in range(n_sub):
        x = x_ref[pl.ds(s * rows, rows), :]
        xb = x.astype(jnp.bfloat16)
        gate = _bdot(xb, wg_ref[...])
        up = _bdot(xb, wu_ref[...])
        h = (jax.nn.silu(gate) * up).astype(jnp.bfloat16)
        d = _bdot(h, wd_ref[...])
        o_ref[pl.ds(s * rows, rows), :] = _layer_norm(ALPHA * x + 0.5 * d, g_ref[...], b_ref[...])


def _const_spec(shape):
    return pl.BlockSpec(shape, lambda i: (0,) * len(shape), pipeline_mode=pl.Buffered(1))


def _ffn(x, wg, wu, wd, g, b, tm, n_sub=1):
    m = x.shape[0]
    assert m % tm == 0
    return pl.pallas_call(
        functools.partial(_ffn_kernel, n_sub=n_sub),
        grid=(m // tm,),
        in_specs=[
            pl.BlockSpec((tm, D_MODEL), lambda i: (i, 0)),
            _const_spec((D_MODEL, D_FF)),
            _const_spec((D_MODEL, D_FF)),
            _const_spec((D_FF, D_MODEL)),
            _const_spec((1, D_MODEL)),
            _const_spec((1, D_MODEL)),
        ],
        out_specs=pl.BlockSpec((tm, D_MODEL), lambda i: (i, 0)),
        out_shape=jax.ShapeDtypeStruct((m, D_MODEL), jnp.float32),
        compiler_params=pltpu.CompilerParams(
            dimension_semantics=("arbitrary",), vmem_limit_bytes=VMEM_LIMIT_BYTES),
        name="ffn_ln",
    )(x, wg, wu, wd, g, b)


def _causal_dw_conv(buf_ref, w_ref, taps, pad, tm):
    base = pad - (taps - 1)
    acc = None
    for r in range(SUBLANES):
        offs = [base + k for k in range(taps) if (base + k) % SUBLANES == r]
        if not offs:
            continue
        rows = tm if r == 0 else tm + SUBLANES
        part = None
        for o in offs:
            term = w_ref[o - base:o - base + 1, :] * buf_ref[pl.ds(o - r, rows), :]
            part = term if part is None else part + term
        part = part[r:r + tm]
        acc = part if acc is None else acc + part
    return acc


def _mix_prompt_kernel(x_ref, win_ref, bin_ref, wa_ref, ba_ref, lcg_ref, lcb_ref,
                       wb_ref, wo_ref, bo_ref, g_ref, b_ref,
                       o_ref, ctxa_ref, ctxb_ref, ubuf, vbuf, *, tm, tiles_per_seq):
    i = pl.program_id(0)

    @pl.when(i % tiles_per_seq == 0)
    def _():
        ubuf[pl.ds(0, PAD_A), :] = jnp.zeros((PAD_A, D_A), jnp.float32)
        vbuf[pl.ds(0, PAD_B), :] = jnp.zeros((PAD_B, D_B), jnp.float32)

    x = x_ref[...]
    z = _bdot(x.astype(jnp.bfloat16), win_ref[...]) + bin_ref[...]
    a_val = z[:, 0:D_A]
    a_gate = z[:, D_A:2 * D_A]
    g_b = z[:, 2 * D_A:2 * D_A + D_B]
    g_c = z[:, 2 * D_A + D_B:2 * D_A + 2 * D_B]
    b_val = z[:, 2 * D_A + 2 * D_B:]
    ubuf[pl.ds(PAD_A, tm), :] = a_val * jax.nn.sigmoid(a_gate)
    vbuf[pl.ds(PAD_B, tm), :] = g_c * b_val

    conv_a = _causal_dw_conv(ubuf, wa_ref, K_A, PAD_A, tm)
    ua = jax.nn.silu(_layer_norm(conv_a + ba_ref[...], lcg_ref[...], lcb_ref[...]))
    ub = g_b * _causal_dw_conv(vbuf, wb_ref, K_B, PAD_B, tm)

    mixed = jnp.concatenate([ua, ub], axis=-1).astype(jnp.bfloat16)
    m = _bdot(mixed, wo_ref[...]) + bo_ref[...]
    o_ref[...] = _layer_norm(ALPHA * x + m, g_ref[...], b_ref[...])

    ctxa_ref[0, 0] = ubuf[pl.ds(PAD_A + tm - (K_A - 1), K_A - 1), :]
    ctxb_ref[0, 0] = vbuf[pl.ds(PAD_B + tm - (K_B - 1), K_B - 1), :]
    ubuf[pl.ds(0, PAD_A), :] = ubuf[pl.ds(tm, PAD_A), :]
    vbuf[pl.ds(0, PAD_B), :] = vbuf[pl.ds(tm, PAD_B), :]


def _mix_prompt(x, n_seq, seq, w_in, b_in, w_dw_a, b_dw_a, lcg, lcb, w_dw_b, w_o, b_o, g, b, tm):
    assert seq % tm == 0 and tm >= PAD_A
    tiles_per_seq = seq // tm
    kern = functools.partial(_mix_prompt_kernel, tm=tm, tiles_per_seq=tiles_per_seq)
    return pl.pallas_call(
        kern,
        grid=(n_seq * tiles_per_seq,),
        in_specs=[
            pl.BlockSpec((tm, D_MODEL), lambda i: (i, 0)),
            _const_spec((D_MODEL, D_IN)),
            _const_spec((1, D_IN)),
            _const_spec((K_A, D_A)),
            _const_spec((1, D_A)),
            _const_spec((1, D_A)),
            _const_spec((1, D_A)),
            _const_spec((K_B, D_B)),
            _const_spec((D_A + D_B, D_MODEL)),
            _const_spec((1, D_MODEL)),
            _const_spec((1, D_MODEL)),
            _const_spec((1, D_MODEL)),
        ],
        out_specs=[
            pl.BlockSpec((tm, D_MODEL), lambda i: (i, 0)),
            pl.BlockSpec((1, 1, K_A - 1, D_A), lambda i: (0, i // tiles_per_seq, 0, 0)),
            pl.BlockSpec((1, 1, K_B - 1, D_B), lambda i: (0, i // tiles_per_seq, 0, 0)),
        ],
        out_shape=[
            jax.ShapeDtypeStruct((n_seq * seq, D_MODEL), jnp.float32),
            jax.ShapeDtypeStruct((1, n_seq, K_A - 1, D_A), jnp.float32),
            jax.ShapeDtypeStruct((1, n_seq, K_B - 1, D_B), jnp.float32),
        ],
        scratch_shapes=[
            pltpu.VMEM((PAD_A + tm, D_A), jnp.float32),
            pltpu.VMEM((PAD_B + tm, D_B), jnp.float32),
        ],
        compiler_params=pltpu.CompilerParams(
            dimension_semantics=("arbitrary",), vmem_limit_bytes=VMEM_LIMIT_BYTES),
        name="mix_prompt",
    )(x, w_in, b_in, w_dw_a, b_dw_a, lcg, lcb, w_dw_b, w_o, b_o, g, b)


def _mix_sample_kernel(x_ref, sa_ref, sb_ref, win_ref, bin_ref, wa_ref, ba_ref, lcg_ref, lcb_ref,
                       wb_ref, wo_ref, bo_ref, g_ref, b_ref,
                       o_ref, nsa_ref, nsb_ref):
    x = x_ref[...]
    z = _bdot(x.astype(jnp.bfloat16), win_ref[...]) + bin_ref[...]
    a_val = z[:, 0:D_A]
    a_gate = z[:, D_A:2 * D_A]
    g_b = z[:, 2 * D_A:2 * D_A + D_B]
    g_c = z[:, 2 * D_A + D_B:2 * D_A + 2 * D_B]
    b_val = z[:, 2 * D_A + 2 * D_B:]
    u = a_val * jax.nn.sigmoid(a_gate)
    v = g_c * b_val

    sa = sa_ref[0]
    conv_a = jnp.sum(sa * wa_ref[0:K_A - 1, :][None], axis=1) + wa_ref[K_A - 1:K_A, :] * u
    ua = jax.nn.silu(_layer_norm(conv_a + ba_ref[...], lcg_ref[...], lcb_ref[...]))

    sb = sb_ref[0]
    conv_b = jnp.sum(sb * wb_ref[0:K_B - 1, :][None], axis=1) + wb_ref[K_B - 1:K_B, :] * v
    ub = g_b * conv_b

    mixed = jnp.concatenate([ua, ub], axis=-1).astype(jnp.bfloat16)
    m = _bdot(mixed, wo_ref[...]) + bo_ref[...]
    o_ref[...] = _layer_norm(ALPHA * x + m, g_ref[...], b_ref[...])

    nsa_ref[0, :, 0:K_A - 2, :] = sa[:, 1:, :]
    nsa_ref[0, :, K_A - 2:K_A - 1, :] = u[:, None, :]
    nsb_ref[0, :, 0:K_B - 2, :] = sb[:, 1:, :]
    nsb_ref[0, :, K_B - 2:K_B - 1, :] = v[:, None, :]


def _mix_sample(x, state_a, state_b, w_in, b_in, w_dw_a, b_dw_a, lcg, lcb, w_dw_b, w_o, b_o, g, b):
    n = x.shape[0]
    vmem = pl.BlockSpec(memory_space=pltpu.VMEM)
    return pl.pallas_call(
        _mix_sample_kernel,
        in_specs=[vmem] * 14,
        out_specs=[vmem] * 3,
        out_shape=[
            jax.ShapeDtypeStruct((n, D_MODEL), jnp.float32),
            jax.ShapeDtypeStruct(state_a.shape, jnp.float32),
            jax.ShapeDtypeStruct(state_b.shape, jnp.float32),
        ],
        compiler_params=pltpu.CompilerParams(vmem_limit_bytes=VMEM_LIMIT_BYTES),
        name="mix_sample",
    )(x, state_a, state_b, w_in, b_in, w_dw_a, b_dw_a, lcg, lcb, w_dw_b, w_o, b_o, g, b)


def kernel(x_prompt, x_sample, state_conv_a, state_conv_b, ln_f1_g, ln_f1_b, f1_wg, f1_wu, f1_wd,
           w_in, b_in, w_dw_a, b_dw_a, ln_conv_g, ln_conv_b, w_dw_b, w_o, b_o,
           ln_mix_g, ln_mix_b, f2_wg, f2_wu, f2_wd, ln_f2_g, ln_f2_b):
    assert f1_wg.shape[0] == DEPTH == 1
    n_seq, seq, _ = x_prompt.shape
    n_dec = x_sample.shape[0]
    bf = jnp.bfloat16
    f1 = (f1_wg[0].astype(bf), f1_wu[0].astype(bf), f1_wd[0].astype(bf), ln_f1_g, ln_f1_b)
    f2 = (f2_wg[0].astype(bf), f2_wu[0].astype(bf), f2_wd[0].astype(bf), ln_f2_g, ln_f2_b)
    mix = (w_in[0].astype(bf), b_in, w_dw_a[0], b_dw_a, ln_conv_g, ln_conv_b, w_dw_b[0],
           w_o[0].astype(bf), b_o, ln_mix_g, ln_mix_b)

    tm = 512
    yp = x_prompt.reshape(n_seq * seq, D_MODEL)
    yp = _ffn(yp, *f1, tm, n_sub=2)
    yp, pa, pb = _mix_prompt(yp, n_seq, seq, *mix, tm)
    yp = _ffn(yp, *f2, tm, n_sub=2)

    ys = x_sample.reshape(n_dec, D_MODEL)
    ys = _ffn(ys, *f1, n_dec)
    ys, sa, sb = _mix_sample(ys, state_conv_a, state_conv_b, *mix)
    ys = _ffn(ys, *f2, n_dec)

    return (yp.reshape(n_seq, seq, D_MODEL), ys.reshape(n_dec, 1, D_MODEL), pa, pb, sa, sb)
```
